```python
import math
import jax, jax.numpy as jnp
from jax import lax
import numpy as np

D_MODEL = 1024
BATCH = 1
SEQ = 16384
DEPTH = 1
DEC_BATCH = 4
DEC_SEQ = 8192
PAST_LEN = 128

D_MIX = D_MODEL
D_ATTN = D_MIX // 2
D_SGU = D_MIX - D_ATTN
N_HEADS = 4
HEAD_DIM = D_ATTN // (2 * N_HEADS)
V_DIM = 2 * HEAD_DIM
N_SGU_GROUPS = 4
SGU_GROUP = D_SGU // N_SGU_GROUPS
CHUNK = 128
Q_BLOCK = 128
ROPE_THETA = 10000.0
EPS = 1e-6
D_IN_PROJ = 3 * D_ATTN + D_ATTN + 2 * D_SGU + D_SGU
SPLITS = tuple(np.cumsum([D_ATTN, D_ATTN, D_ATTN, D_ATTN, D_SGU, D_SGU])[:].tolist())

kernel_name = "hybrid_diffattn_sgu_encoder"


def _lambda_init(layer_idx):
    return 0.8 - 0.6 * math.exp(-0.3 * layer_idx)


def rms_norm(x, w):
    xf = x.astype(jnp.float32)
    y = xf * lax.rsqrt(jnp.mean(xf * xf, axis=-1, keepdims=True) + EPS)
    return (y * w.astype(jnp.float32)).astype(x.dtype)


def layer_norm(x, w, b):
    xf = x.astype(jnp.float32)
    mu = jnp.mean(xf, axis=-1, keepdims=True)
    var = jnp.mean(jnp.square(xf - mu), axis=-1, keepdims=True)
    y = (xf - mu) * lax.rsqrt(var + EPS)
    return (y * w.astype(jnp.float32) + b.astype(jnp.float32)).astype(x.dtype)


def rope(x, seq_len):
    half = HEAD_DIM // 2
    inv_freq = 1.0 / (ROPE_THETA ** (jnp.arange(half, dtype=jnp.float32) / half))
    ang = jnp.arange(seq_len, dtype=jnp.float32)[:, None] * inv_freq[None, :]
    ang = jnp.concatenate([ang, ang], axis=-1)
    cos = jnp.cos(ang)[None, :, None, :].astype(x.dtype)
    sin = jnp.sin(ang)[None, :, None, :].astype(x.dtype)
    x1, x2 = x[..., :half], x[..., half:]
    rot = jnp.concatenate([-x2, x1], axis=-1)
    return x * cos + rot * sin


def diff_attention(q, k, v, q_norm_w, k_norm_w, lambda_qk, subln_w, lambda_init):
    B, S, _ = q.shape
    q = q.reshape(B, S, 2 * N_HEADS, HEAD_DIM)
    k = k.reshape(B, S, 2 * N_HEADS, HEAD_DIM)
    v = v.reshape(B, S, N_HEADS, V_DIM)
    q = rope(rms_norm(q, q_norm_w), S) * (HEAD_DIM ** -0.5)
    k = rope(rms_norm(k, k_norm_w), S)
    lq = lambda_qk.astype(jnp.float32)
    lam = jnp.exp(jnp.sum(lq[0] * lq[1])) - jnp.exp(jnp.sum(lq[2] * lq[3])) + lambda_init
    nb = S // Q_BLOCK
    qb = q.reshape(B, nb, Q_BLOCK, 2 * N_HEADS, HEAD_DIM).transpose(1, 0, 3, 2, 4)
    kt = k.transpose(0, 2, 1, 3)
    vt = v.transpose(0, 2, 1, 3)

    def block(q_blk):
        s = jnp.einsum('bhqd,bhkd->bhqk', q_blk, kt).astype(jnp.float32)
        p = jax.nn.softmax(s, axis=-1).reshape(B, N_HEADS, 2, Q_BLOCK, S)
        p_diff = (p[:, :, 0] - lam * p[:, :, 1]).astype(vt.dtype)
        return jnp.einsum('bhqk,bhkv->bhqv', p_diff, vt)

    o = lax.map(block, qb)
    o = o.transpose(1, 0, 3, 2, 4).reshape(B, S, N_HEADS, V_DIM)
    o = rms_norm(o, subln_w) * (1.0 - lambda_init)
    return o.reshape(B, S, D_ATTN)


def spatial_gating(u, vg, sgu_norm_w, sgu_norm_b, w_spatial, b_spatial):
    B, S, _ = u.shape
    vn = layer_norm(vg, sgu_norm_w, sgu_norm_b)
    vc = vn.reshape(B, S // CHUNK, CHUNK, N_SGU_GROUPS, SGU_GROUP)
    mixed = jnp.einsum('gij,bnjgc->bnigc', w_spatial, vc) + b_spatial.T[None, None, :, :, None]
    return u * mixed.reshape(B, S, D_SGU)


def hybrid_layer(x, c, norm_w, w_ada, b_ada, w_in, w_out, q_norm_w, k_norm_w, lambda_qk,
                 subln_w, sgu_norm_w, sgu_norm_b, w_spatial, b_spatial, lambda_init):
    mod = jnp.einsum('bd,de->be', jax.nn.silu(c), w_ada) + b_ada
    shift, scale, gate = jnp.split(mod, 3, axis=-1)
    h = rms_norm(x, norm_w) * (1.0 + scale[:, None, :]) + shift[:, None, :]
    proj = jnp.einsum('bsd,de->bse', h, w_in)
    q, k, v, z_a, u, vg, z_s = jnp.split(proj, SPLITS, axis=-1)
    a = diff_attention(q, k, v, q_norm_w, k_norm_w, lambda_qk, subln_w, lambda_init) * jax.nn.silu(z_a)
    s = spatial_gating(u, vg, sgu_norm_w, sgu_norm_b, w_spatial, b_spatial) * jax.nn.silu(z_s)
    y = jnp.einsum('bse,ed->bsd', jnp.concatenate([a, s], axis=-1), w_out)
    return x + gate[:, None, :] * y


def setup_inputs(seed: int = 0) -> dict:
    key = jax.random.key(seed)
    ks = jax.random.split(key, 18)
    f32 = jnp.float32
    nrm = lambda k, shape, s: jax.random.normal(k, shape, f32) * s
    return {
        "x_prompt": nrm(ks[0], (BATCH, SEQ, D_MODEL), 1.0),
        "x_sample": nrm(ks[1], (DEC_BATCH, DEC_SEQ, D_MODEL), 1.0),
        "c_prompt": nrm(ks[2], (BATCH, D_MODEL), 1.0),
        "c_sample": nrm(ks[3], (DEC_BATCH, D_MODEL), 1.0),
        "norm_w": 1.0 + nrm(ks[4], (DEPTH, D_MODEL), 0.02),
        "w_ada": nrm(ks[5], (DEPTH, D_MODEL, 3 * D_MODEL), D_MODEL ** -0.5),
        "b_ada": nrm(ks[6], (DEPTH, 3 * D_MODEL), 0.01),
        "w_in": nrm(ks[7], (DEPTH, D_MODEL, D_IN_PROJ), D_MODEL ** -0.5),
        "w_out": nrm(ks[8], (DEPTH, D_MIX, D_MODEL), D_MIX ** -0.5),
        "q_norm_w": 1.0 + nrm(ks[9], (DEPTH, HEAD_DIM), 0.02),
        "k_norm_w": 1.0 + nrm(ks[10], (DEPTH, HEAD_DIM), 0.02),
        "lambda_qk": nrm(ks[11], (DEPTH, 4, HEAD_DIM), 0.1),
        "subln_w": 1.0 + nrm(ks[12], (DEPTH, V_DIM), 0.02),
        "sgu_norm_w": 1.0 + nrm(ks[13], (DEPTH, D_SGU), 0.02),
        "sgu_norm_b": nrm(ks[14], (DEPTH, D_SGU), 0.01),
        "w_spatial": nrm(ks[15], (DEPTH, N_SGU_GROUPS, CHUNK, CHUNK), CHUNK ** -0.5),
        "b_spatial": 1.0 + nrm(ks[16], (DEPTH, N_SGU_GROUPS, CHUNK), 0.02),
    }


def reference(x_prompt, x_sample, c_prompt, c_sample, norm_w, w_ada, b_ada, w_in, w_out,
              q_norm_w, k_norm_w, lambda_qk, subln_w, sgu_norm_w, sgu_norm_b, w_spatial, b_spatial):
    y_prompt = x_prompt
    y_sample = x_sample
    for l in range(DEPTH):
        lam0 = _lambda_init(l)
        params = (norm_w[l], w_ada[l], b_ada[l], w_in[l], w_out[l], q_norm_w[l], k_norm_w[l],
                  lambda_qk[l], subln_w[l], sgu_norm_w[l], sgu_norm_b[l], w_spatial[l], b_spatial[l])
        y_prompt = hybrid_layer(y_prompt, c_prompt, *params, lam0)
        y_sample = hybrid_layer(y_sample, c_sample, *params, lam0)
    return (y_prompt, y_sample)
```

```python
import functools
import math

import jax
import jax.numpy as jnp
from jax import lax
from jax.experimental import pallas as pl
from jax.experimental.pallas import tpu as pltpu

D_MODEL = 1024
D_ATTN = 512
D_SGU = 512
N_HEADS = 4
HEAD_DIM = 64
V_DIM = 128
N_SGU_GROUPS = 4
CHUNK = 128
ROPE_THETA = 10000.0
EPS = 1e-6
LAMBDA_INIT = 0.8 - 0.6 * math.exp(-0.3 * 0)
LOG2E = 1.4426950408889634

LANES = 128
ROW_TILE = 512
Q_TILE = 512
KV_TILE = 512
VMEM_LIMIT = 48 * 1024 * 1024
NEG_BIG = -1e30

F32 = jnp.float32
BF16 = jnp.bfloat16


def _sigmoid(z):
    return 1.0 / (1.0 + jnp.exp(-z))


def _ada_kernel(c_ref, w_ref, b_ref, o_ref):
    c = c_ref[...]
    a = (c * _sigmoid(c)).astype(BF16)
    o_ref[...] = jnp.dot(a, w_ref[...].astype(BF16), preferred_element_type=F32) + b_ref[...]


def _ada(c_all, w_ada, b_ada):
    rows = c_all.shape[0]
    n = w_ada.shape[1]
    tn = 1024
    return pl.pallas_call(
        _ada_kernel,
        grid=(n // tn,),
        in_specs=[
            pl.BlockSpec((rows, D_MODEL), lambda j: (0, 0)),
            pl.BlockSpec((D_MODEL, tn), lambda j: (0, j)),
            pl.BlockSpec((1, tn), lambda j: (0, j)),
        ],
        out_specs=pl.BlockSpec((rows, tn), lambda j: (0, j)),
        out_shape=jax.ShapeDtypeStruct((rows, n), F32),
        name="ada",
    )(c_all, w_ada, b_ada.reshape(1, n))


def _chunk_rms(t, g_ref, w):
    sq = t * t
    hi = sq.astype(BF16)
    lo = (sq - hi.astype(F32)).astype(BF16)
    ss = (jnp.dot(hi, g_ref[...], preferred_element_type=F32)
          + jnp.dot(lo, g_ref[...], preferred_element_type=F32))
    return t * lax.rsqrt(ss * (1.0 / HEAD_DIM) + EPS) * w


def _rope_slab(t, cos, sin_lo, sin_hi):
    fwd = pltpu.roll(t, LANES - HEAD_DIM // 2, 1)
    bwd = pltpu.roll(t, HEAD_DIM // 2, 1)
    return t * cos + fwd * sin_lo + bwd * sin_hi


def _pre_kernel(x_ref, mod_ref, nw_ref, win_ref, qnw_ref, knw_ref, cos_ref, slo_ref, shi_ref,
                g_ref, sgw_ref, sgb_ref, wsp_ref, bsp_ref,
                qlo_ref, qhi_ref, k_ref, vt_ref, ga_ref, s_ref):
    ts = x_ref.shape[1]
    x = x_ref[0]
    shift = mod_ref[0, :, 0:D_MODEL]
    scale = mod_ref[0, :, D_MODEL:2 * D_MODEL]
    xn = x * lax.rsqrt(jnp.mean(x * x, axis=-1, keepdims=True) + EPS) * nw_ref[...]
    hb = (xn * (1.0 + scale) + shift).astype(BF16)

    def proj(idx):
        return jnp.dot(hb, win_ref[:, idx * 512:(idx + 1) * 512], preferred_element_type=F32)

    cos = cos_ref[...]
    slo = slo_ref[...]
    shi = shi_ref[...]
    lane = lax.broadcasted_iota(jnp.int32, (ts, LANES), 1)
    first_map = lane < HEAD_DIM

    qn = _chunk_rms(proj(0), g_ref, qnw_ref[...])
    for h in range(N_HEADS):
        qr = _rope_slab(qn[:, h * LANES:(h + 1) * LANES], cos, slo, shi) * (HEAD_DIM ** -0.5 * LOG2E)
        qlo_ref[0, h] = jnp.where(first_map, qr, 0.0).astype(BF16)
        qhi_ref[0, h] = jnp.where(first_map, 0.0, qr).astype(BF16)

    kn = _chunk_rms(proj(1), g_ref, knw_ref[...])
    for h in range(N_HEADS):
        k_ref[0, h] = _rope_slab(kn[:, h * LANES:(h + 1) * LANES], cos, slo, shi).astype(BF16)

    v = proj(2)
    for h in range(N_HEADS):
        vt_ref[0, h] = v[:, h * LANES:(h + 1) * LANES].T.astype(BF16)

    za = proj(3)
    ga_ref[0] = (za * _sigmoid(za)).astype(BF16)

    u = proj(4)
    vg = proj(5)
    zs = proj(6)
    mu = jnp.mean(vg, axis=-1, keepdims=True)
    vc = vg - mu
    var = jnp.mean(vc * vc, axis=-1, keepdims=True)
    vn = (vc * lax.rsqrt(var + EPS) * sgw_ref[...] + sgb_ref[...]).astype(BF16)
    gate = u * (zs * _sigmoid(zs))
    nc = ts // CHUNK
    for g in range(N_SGU_GROUPS):
        gs = slice(g * LANES, (g + 1) * LANES)
        xg = jnp.concatenate([vn[c * CHUNK:(c + 1) * CHUNK, gs] for c in range(nc)], axis=1)
        mg = jnp.dot(wsp_ref[g], xg, preferred_element_type=F32)
        bias = bsp_ref[g]
        for c in range(nc):
            rs = slice(c * CHUNK, (c + 1) * CHUNK)
            s_ref[0, rs, gs] = (gate[rs, gs] * (mg[:, c * LANES:(c + 1) * LANES] + bias)).astype(BF16)


def _pre(x, mod, p, cos, slo, shi):
    b, s, _ = x.shape
    ts = ROW_TILE
    full = lambda shape: pl.BlockSpec(shape, lambda bi, i: (0,) * len(shape))
    head_out = lambda: pl.BlockSpec((1, N_HEADS, ts, LANES), lambda bi, i: (bi, 0, i, 0))
    return pl.pallas_call(
        _pre_kernel,
        grid=(b, s // ts),
        in_specs=[
            pl.BlockSpec((1, ts, D_MODEL), lambda bi, i: (bi, i, 0)),
            pl.BlockSpec((1, 1, 3 * D_MODEL), lambda bi, i: (bi, 0, 0)),
            full((1, D_MODEL)),
            full((D_MODEL, 7 * 512)),
            full((1, D_ATTN)),
            full((1, D_ATTN)),
            pl.BlockSpec((ts, LANES), lambda bi, i: (i, 0)),
            pl.BlockSpec((ts, LANES), lambda bi, i: (i, 0)),
            pl.BlockSpec((ts, LANES), lambda bi, i: (i, 0)),
            full((D_ATTN, D_ATTN)),
            full((1, D_SGU)),
            full((1, D_SGU)),
            full((N_SGU_GROUPS, CHUNK, CHUNK)),
            full((N_SGU_GROUPS, CHUNK, LANES)),
        ],
        out_specs=[
            head_out(),
            head_out(),
            head_out(),
            pl.BlockSpec((1, N_HEADS, V_DIM, ts), lambda bi, i: (bi, 0, 0, i)),
            pl.BlockSpec((1, ts, D_ATTN), lambda bi, i: (bi, i, 0)),
            pl.BlockSpec((1, ts, D_SGU), lambda bi, i: (bi, i, 0)),
        ],
        out_shape=[
            jax.ShapeDtypeStruct((b, N_HEADS, s, LANES), BF16),
            jax.ShapeDtypeStruct((b, N_HEADS, s, LANES), BF16),
            jax.ShapeDtypeStruct((b, N_HEADS, s, LANES), BF16),
            jax.ShapeDtypeStruct((b, N_HEADS, V_DIM, s), BF16),
            jax.ShapeDtypeStruct((b, s, D_ATTN), BF16),
            jax.ShapeDtypeStruct((b, s, D_SGU), BF16),
        ],
        compiler_params=pltpu.CompilerParams(
            dimension_semantics=("arbitrary", "arbitrary"), vmem_limit_bytes=VMEM_LIMIT),
        name="pre",
    )(x, mod, p["norm_w"], p["w_in"], p["q_norm_w"], p["k_norm_w"], cos, slo, shi,
      p["chunk_ones"], p["sgu_norm_w"], p["sgu_norm_b"], p["w_spatial"], p["b_spatial"])


def _attn_kernel(qlo_ref, qhi_ref, k_ref, vt_ref, ga_ref, lqk_ref, subw_ref, o_ref,
                 acc_ref, m_ref, l_ref):
    tq = qlo_ref.shape[2]
    s_len = k_ref.shape[2]
    tk = KV_TILE
    qd = jnp.concatenate([qlo_ref[0, 0], qhi_ref[0, 0]], axis=0)
    m_ref[...] = jnp.full(m_ref.shape, NEG_BIG, F32)
    l_ref[...] = jnp.zeros(l_ref.shape, F32)
    acc_ref[...] = jnp.zeros(acc_ref.shape, F32)

    def body(j, carry):
        off = pl.multiple_of(j * tk, tk)
        kb = k_ref[0, 0, pl.ds(off, tk), :]
        s = lax.dot_general(kb, qd, (((1,), (1,)), ((), ())),
                            preferred_element_type=F32)
        m_old = m_ref[...]
        m_new = jnp.maximum(m_old, jnp.max(s, axis=0, keepdims=True))
        alpha = jnp.exp2(m_old - m_new)
        p = jnp.exp2(s - m_new)
        l_ref[...] = alpha * l_ref[...] + jnp.sum(p, axis=0, keepdims=True)
        vtb = vt_ref[0, 0, :, pl.ds(off, tk)]
        pv = jnp.dot(vtb, p.astype(BF16), preferred_element_type=F32)
        acc_ref[...] = alpha * acc_ref[...] + pv
        m_ref[...] = m_new
        return carry

    lax.fori_loop(0, s_len // tk, body, 0)

    lq = lqk_ref[...]
    lam = (jnp.exp(jnp.sum(lq[0:1] * lq[1:2], axis=1, keepdims=True))
           - jnp.exp(jnp.sum(lq[2:3] * lq[3:4], axis=1, keepdims=True)) + LAMBDA_INIT)
    on_all = acc_ref[...] * (1.0 / l_ref[...])
    o = on_all[:, :tq] - lam * on_all[:, tq:]
    ms = jnp.mean(o * o, axis=0, keepdims=True)
    subw = jnp.concatenate([subw_ref[...]] * (tq // LANES), axis=1)
    on = o * lax.rsqrt(ms + EPS) * subw * (1.0 - LAMBDA_INIT)
    o_ref[0] = (on.T * ga_ref[0].astype(F32)).astype(BF16)


def _attn(qlo, qhi, k, vt, ga, lqk, subw):
    b, _, s, _ = k.shape
    tq = Q_TILE
    q_spec = lambda: pl.BlockSpec((1, 1, tq, LANES), lambda bi, h, i: (bi, h, i, 0))
    return pl.pallas_call(
        _attn_kernel,
        grid=(b, N_HEADS, s // tq),
        in_specs=[
            q_spec(),
            q_spec(),
            pl.BlockSpec((1, 1, s, LANES), lambda bi, h, i: (bi, h, 0, 0)),
            pl.BlockSpec((1, 1, V_DIM, s), lambda bi, h, i: (bi, h, 0, 0)),
            pl.BlockSpec((1, tq, V_DIM), lambda bi, h, i: (bi, i, h)),
            pl.BlockSpec((4, HEAD_DIM), lambda bi, h, i: (0, 0)),
            pl.BlockSpec((V_DIM, LANES), lambda bi, h, i: (0, 0)),
        ],
        out_specs=pl.BlockSpec((1, tq, V_DIM), lambda bi, h, i: (bi, i, h)),
        out_shape=jax.ShapeDtypeStruct((b, s, D_ATTN), BF16),
        scratch_shapes=[
            pltpu.VMEM((V_DIM, 2 * tq), F32),
            pltpu.VMEM((1, 2 * tq), F32),
            pltpu.VMEM((1, 2 * tq), F32),
        ],
        compiler_params=pltpu.CompilerParams(
            dimension_semantics=("arbitrary", "arbitrary", "arbitrary"),
            vmem_limit_bytes=VMEM_LIMIT),
        name="attn",
    )(qlo, qhi, k, vt, ga, lqk, subw)


def _post_kernel(x_ref, a_ref, s_ref, mod_ref, wout_ref, o_ref):
    y = (jnp.dot(a_ref[0], wout_ref[0:D_ATTN, :], preferred_element_type=F32)
         + jnp.dot(s_ref[0], wout_ref[D_ATTN:, :], preferred_element_type=F32))
    gate = mod_ref[0, :, 2 * D_MODEL:3 * D_MODEL]
    o_ref[0] = x_ref[0] + gate * y


def _post(x, a, sg, mod, w_out):
    b, s, _ = x.shape
    ts = ROW_TILE
    return pl.pallas_call(
        _post_kernel,
        grid=(b, s // ts),
        in_specs=[
            pl.BlockSpec((1, ts, D_MODEL), lambda bi, i: (bi, i, 0)),
            pl.BlockSpec((1, ts, D_ATTN), lambda bi, i: (bi, i, 0)),
            pl.BlockSpec((1, ts, D_SGU), lambda bi, i: (bi, i, 0)),
            pl.BlockSpec((1, 1, 3 * D_MODEL), lambda bi, i: (bi, 0, 0)),
            pl.BlockSpec((D_MODEL, D_MODEL), lambda bi, i: (0, 0)),
        ],
        out_specs=pl.BlockSpec((1, ts, D_MODEL), lambda bi, i: (bi, i, 0)),
        out_shape=jax.ShapeDtypeStruct((b, s, D_MODEL), F32),
        compiler_params=pltpu.CompilerParams(
            dimension_semantics=("arbitrary", "arbitrary"), vmem_limit_bytes=VMEM_LIMIT),
        name="post",
    )(x, a, sg, mod, w_out)


def _rope_tables(seq_len):
    half = HEAD_DIM // 2
    inv_freq = 1.0 / (ROPE_THETA ** (jnp.arange(half, dtype=F32) / half))
    ang = jnp.arange(seq_len, dtype=F32)[:, None] * inv_freq[None, :]
    ang = jnp.concatenate([ang, ang, ang, ang], axis=-1)
    cos = jnp.cos(ang)
    sin = jnp.sin(ang)
    first_half = (jnp.arange(LANES) % HEAD_DIM) < half
    sin_lo = jnp.where(first_half[None, :], -sin, 0.0)
    sin_hi = jnp.where(first_half[None, :], 0.0, sin)
    return cos, sin_lo, sin_hi


def _layer(x, mod, p):
    cos, slo, shi = _rope_tables(x.shape[1])
    qlo, qhi, k, vt, ga, sg = _pre(x, mod, p, cos, slo, shi)
    a = _attn(qlo, qhi, k, vt, ga, p["lambda_qk"], p["subln_w"])
    return _post(x, a, sg, mod, p["w_out"])


def kernel(x_prompt, x_sample, c_prompt, c_sample, norm_w, w_ada, b_ada, w_in, w_out, q_norm_w, k_norm_w, lambda_qk, subln_w, sgu_norm_w, sgu_norm_b, w_spatial, b_spatial):
    depth = norm_w.shape[0]
    assert depth == 1
    nb_p = c_prompt.shape[0]
    nb_s = c_sample.shape[0]
    rows = -(-(nb_p + nb_s) // 8) * 8
    c_all = jnp.concatenate(
        [c_prompt, c_sample, jnp.zeros((rows - nb_p - nb_s, D_MODEL), F32)], axis=0)
    y_prompt, y_sample = x_prompt, x_sample
    for l in range(depth):
        p = {
            "norm_w": norm_w[l].reshape(1, D_MODEL),
            "w_in": w_in[l].astype(BF16),
            "w_out": w_out[l].astype(BF16),
            "q_norm_w": jnp.tile(q_norm_w[l], D_ATTN // HEAD_DIM).reshape(1, D_ATTN),
            "k_norm_w": jnp.tile(k_norm_w[l], D_ATTN // HEAD_DIM).reshape(1, D_ATTN),
            "lambda_qk": lambda_qk[l],
            "subln_w": jnp.broadcast_to(subln_w[l][:, None], (V_DIM, LANES)),
            "sgu_norm_w": sgu_norm_w[l].reshape(1, D_SGU),
            "sgu_norm_b": sgu_norm_b[l].reshape(1, D_SGU),
            "w_spatial": w_spatial[l].astype(BF16),
            "b_spatial": jnp.broadcast_to(b_spatial[l][:, :, None], (N_SGU_GROUPS, CHUNK, LANES)),
            "chunk_ones": jnp.kron(jnp.eye(D_ATTN // HEAD_DIM, dtype=F32),
                                   jnp.ones((HEAD_DIM, HEAD_DIM), F32)).astype(BF16),
        }
        mod = _ada(c_all, w_ada[l], b_ada[l])
        mod_p = mod[0:nb_p].reshape(nb_p, 1, 3 * D_MODEL)
        mod_s = mod[nb_p:nb_p + nb_s].reshape(nb_s, 1, 3 * D_MODEL)
        y_prompt = _layer(y_prompt, mod_p, p)
        y_sample = _layer(y_sample, mod_s, p)
    return (y_prompt, y_sample)
```

```python
import functools
import math

import jax
import jax.numpy as jnp
from jax import lax
from jax.experimental import pallas as pl
from jax.experimental.pallas import tpu as pltpu

D_MODEL = 1024
D_ATTN = 512
D_SGU = 512
N_HEADS = 4
HEAD_DIM = 64
V_DIM = 128
N_SGU_GROUPS = 4
CHUNK = 128
ROPE_THETA = 10000.0
EPS = 1e-6
LAMBDA_INIT = 0.8 - 0.6 * math.exp(-0.3 * 0)
LOG2E = 1.4426950408889634

LANES = 128
BF16_SUBLANES = 16
V_ROWS = V_DIM + BF16_SUBLANES
ROW_TILE = 512
Q_TILE = 512
KV_TILE = 512
KV_UNROLL = 4
VMEM_LIMIT = 48 * 1024 * 1024
NEG_BIG = -1e30

F32 = jnp.float32
BF16 = jnp.bfloat16


def _sigmoid(z):
    return 1.0 / (1.0 + jnp.exp(-z))


def _ada_kernel(c_ref, w_ref, b_ref, o_ref):
    c = c_ref[...]
    a = (c * _sigmoid(c)).astype(BF16)
    o_ref[...] = jnp.dot(a, w_ref[...].astype(BF16), preferred_element_type=F32) + b_ref[...]


def _ada(c_all, w_ada, b_ada):
    rows = c_all.shape[0]
    n = w_ada.shape[1]
    tn = 1024
    return pl.pallas_call(
        _ada_kernel,
        grid=(n // tn,),
        in_specs=[
            pl.BlockSpec((rows, D_MODEL), lambda j: (0, 0)),
            pl.BlockSpec((D_MODEL, tn), lambda j: (0, j)),
            pl.BlockSpec((1, tn), lambda j: (0, j)),
        ],
        out_specs=pl.BlockSpec((rows, tn), lambda j: (0, j)),
        out_shape=jax.ShapeDtypeStruct((rows, n), F32),
        name="ada",
    )(c_all, w_ada, b_ada.reshape(1, n))


def _chunk_rms(t, g_ref, w):
    sq = t * t
    hi = sq.astype(BF16)
    lo = (sq - hi.astype(F32)).astype(BF16)
    ss = (jnp.dot(hi, g_ref[...], preferred_element_type=F32)
          + jnp.dot(lo, g_ref[...], preferred_element_type=F32))
    return t * lax.rsqrt(ss * (1.0 / HEAD_DIM) + EPS) * w


def _rope_slab(t, cos, sin_lo, sin_hi):
    fwd = pltpu.roll(t, LANES - HEAD_DIM // 2, 1)
    bwd = pltpu.roll(t, HEAD_DIM // 2, 1)
    return t * cos + fwd * sin_lo + bwd * sin_hi


def _pre_kernel(x_ref, mod_ref, nw_ref, win_ref, qnw_ref, knw_ref, cos_ref, slo_ref, shi_ref,
                g_ref, sgw_ref, sgb_ref, wsp_ref, bsp_ref,
                qlo_ref, qhi_ref, k_ref, vt_ref, ga_ref, s_ref):
    ts = x_ref.shape[1]
    x = x_ref[0]
    shift = mod_ref[0, :, 0:D_MODEL]
    scale = mod_ref[0, :, D_MODEL:2 * D_MODEL]
    xn = x * lax.rsqrt(jnp.mean(x * x, axis=-1, keepdims=True) + EPS) * nw_ref[...]
    hb = (xn * (1.0 + scale) + shift).astype(BF16)

    def proj(idx):
        return jnp.dot(hb, win_ref[:, idx * 512:(idx + 1) * 512], preferred_element_type=F32)

    cos = cos_ref[...]
    slo = slo_ref[...]
    shi = shi_ref[...]
    lane = lax.broadcasted_iota(jnp.int32, (ts, LANES), 1)
    first_map = lane < HEAD_DIM

    qn = _chunk_rms(proj(0), g_ref, qnw_ref[...])
    for h in range(N_HEADS):
        qr = _rope_slab(qn[:, h * LANES:(h + 1) * LANES], cos, slo, shi) * (HEAD_DIM ** -0.5 * LOG2E)
        qlo_ref[0, h] = jnp.where(first_map, qr, 0.0).astype(BF16)
        qhi_ref[0, h] = jnp.where(first_map, 0.0, qr).astype(BF16)

    kn = _chunk_rms(proj(1), g_ref, knw_ref[...])
    for h in range(N_HEADS):
        k_ref[0, h] = _rope_slab(kn[:, h * LANES:(h + 1) * LANES], cos, slo, shi).astype(BF16)

    v = proj(2)
    for h in range(N_HEADS):
        vt_ref[0, h, 0:V_DIM, :] = v[:, h * LANES:(h + 1) * LANES].T.astype(BF16)
        vt_ref[0, h, V_DIM:V_ROWS, :] = jnp.ones((V_ROWS - V_DIM, ts), BF16)

    za = proj(3)
    ga_ref[0] = (za * _sigmoid(za)).astype(BF16)

    u = proj(4)
    vg = proj(5)
    zs = proj(6)
    mu = jnp.mean(vg, axis=-1, keepdims=True)
    vc = vg - mu
    var = jnp.mean(vc * vc, axis=-1, keepdims=True)
    vn = (vc * lax.rsqrt(var + EPS) * sgw_ref[...] + sgb_ref[...]).astype(BF16)
    gate = u * (zs * _sigmoid(zs))
    nc = ts // CHUNK
    for g in range(N_SGU_GROUPS):
        gs = slice(g * LANES, (g + 1) * LANES)
        xg = jnp.concatenate([vn[c * CHUNK:(c + 1) * CHUNK, gs] for c in range(nc)], axis=1)
        mg = jnp.dot(wsp_ref[g], xg, preferred_element_type=F32)
        bias = bsp_ref[g]
        for c in range(nc):
            rs = slice(c * CHUNK, (c + 1) * CHUNK)
            s_ref[0, rs, gs] = (gate[rs, gs] * (mg[:, c * LANES:(c + 1) * LANES] + bias)).astype(BF16)


def _pre(x, mod, p, cos, slo, shi):
    b, s, _ = x.shape
    ts = ROW_TILE
    full = lambda shape: pl.BlockSpec(shape, lambda bi, i: (0,) * len(shape))
    head_out = lambda: pl.BlockSpec((1, N_HEADS, ts, LANES), lambda bi, i: (bi, 0, i, 0))
    return pl.pallas_call(
        _pre_kernel,
        grid=(b, s // ts),
        in_specs=[
            pl.BlockSpec((1, ts, D_MODEL), lambda bi, i: (bi, i, 0)),
            pl.BlockSpec((1, 1, 3 * D_MODEL), lambda bi, i: (bi, 0, 0)),
            full((1, D_MODEL)),
            full((D_MODEL, 7 * 512)),
            full((1, D_ATTN)),
            full((1, D_ATTN)),
            pl.BlockSpec((ts, LANES), lambda bi, i: (i, 0)),
            pl.BlockSpec((ts, LANES), lambda bi, i: (i, 0)),
            pl.BlockSpec((ts, LANES), lambda bi, i: (i, 0)),
            full((D_ATTN, D_ATTN)),
            full((1, D_SGU)),
            full((1, D_SGU)),
            full((N_SGU_GROUPS, CHUNK, CHUNK)),
            full((N_SGU_GROUPS, CHUNK, LANES)),
        ],
        out_specs=[
            head_out(),
            head_out(),
            head_out(),
            pl.BlockSpec((1, N_HEADS, V_ROWS, ts), lambda bi, i: (bi, 0, 0, i)),
            pl.BlockSpec((1, ts, D_ATTN), lambda bi, i: (bi, i, 0)),
            pl.BlockSpec((1, ts, D_SGU), lambda bi, i: (bi, i, 0)),
        ],
        out_shape=[
            jax.ShapeDtypeStruct((b, N_HEADS, s, LANES), BF16),
            jax.ShapeDtypeStruct((b, N_HEADS, s, LANES), BF16),
            jax.ShapeDtypeStruct((b, N_HEADS, s, LANES), BF16),
            jax.ShapeDtypeStruct((b, N_HEADS, V_ROWS, s), BF16),
            jax.ShapeDtypeStruct((b, s, D_ATTN), BF16),
            jax.ShapeDtypeStruct((b, s, D_SGU), BF16),
        ],
        compiler_params=pltpu.CompilerParams(
            dimension_semantics=("arbitrary", "arbitrary"), vmem_limit_bytes=VMEM_LIMIT),
        name="pre",
    )(x, mod, p["norm_w"], p["w_in"], p["q_norm_w"], p["k_norm_w"], cos, slo, shi,
      p["chunk_ones"], p["sgu_norm_w"], p["sgu_norm_b"], p["w_spatial"], p["b_spatial"])


def _attn_kernel(qlo_ref, qhi_ref, k_ref, vt_ref, ga_ref, lqk_ref, subw_ref, o_ref,
                 acc_ref, m_ref, sa_ref, sb_ref):
    tq = qlo_ref.shape[2]
    s_len = k_ref.shape[2]
    tk = KV_TILE
    n_blocks = s_len // tk
    n_iters = n_blocks // KV_UNROLL
    qd = jnp.concatenate([qlo_ref[0, 0], qhi_ref[0, 0]], axis=0)
    m_ref[...] = jnp.full(m_ref.shape, NEG_BIG, F32)
    acc_ref[...] = jnp.zeros(acc_ref.shape, F32)

    def scores(blk):
        off = pl.multiple_of(blk * tk, tk)
        kb = k_ref[0, 0, pl.ds(off, tk), :]
        return lax.dot_general(kb, qd, (((1,), (1,)), ((), ())),
                               preferred_element_type=F32)

    def consume(s_ref, blk):
        off = pl.multiple_of(blk * tk, tk)
        s = s_ref[...]
        m_old = m_ref[...]
        m_new = jnp.maximum(m_old, jnp.max(s, axis=0, keepdims=True))
        alpha = jnp.exp2(m_old - m_new)
        p = jnp.exp2(s - m_new)
        vtb = vt_ref[0, 0, :, pl.ds(off, tk)]
        pv = jnp.dot(vtb, p.astype(BF16), preferred_element_type=F32)
        acc_ref[...] = alpha * acc_ref[...] + pv
        m_ref[...] = m_new

    bufs = (sa_ref, sb_ref)
    sa_ref[...] = scores(0)

    def step(base, last):
        for u in range(KV_UNROLL):
            if not (last and u == KV_UNROLL - 1):
                bufs[(u + 1) % 2][...] = scores(base + u + 1)
            consume(bufs[u % 2], base + u)

    def body(it, carry):
        step(it * KV_UNROLL, False)
        return carry

    lax.fori_loop(0, n_iters - 1, body, 0)
    step((n_iters - 1) * KV_UNROLL, True)

    lq = lqk_ref[...]
    lam = (jnp.exp(jnp.sum(lq[0:1] * lq[1:2], axis=1, keepdims=True))
           - jnp.exp(jnp.sum(lq[2:3] * lq[3:4], axis=1, keepdims=True)) + LAMBDA_INIT)
    acc = acc_ref[...]
    on_all = acc[0:V_DIM] * (1.0 / acc[V_DIM:V_DIM + 1])
    o = on_all[:, :tq] - lam * on_all[:, tq:]
    ms = jnp.mean(o * o, axis=0, keepdims=True)
    subw = jnp.concatenate([subw_ref[...]] * (tq // LANES), axis=1)
    on = o * lax.rsqrt(ms + EPS) * subw * (1.0 - LAMBDA_INIT)
    o_ref[0] = (on.T * ga_ref[0].astype(F32)).astype(BF16)


def _attn(qlo, qhi, k, vt, ga, lqk, subw):
    b, _, s, _ = k.shape
    tq = Q_TILE
    q_spec = lambda: pl.BlockSpec((1, 1, tq, LANES), lambda bi, h, i: (bi, h, i, 0))
    return pl.pallas_call(
        _attn_kernel,
        grid=(b, N_HEADS, s // tq),
        in_specs=[
            q_spec(),
            q_spec(),
            pl.BlockSpec((1, 1, s, LANES), lambda bi, h, i: (bi, h, 0, 0)),
            pl.BlockSpec((1, 1, V_ROWS, s), lambda bi, h, i: (bi, h, 0, 0)),
            pl.BlockSpec((1, tq, V_DIM), lambda bi, h, i: (bi, i, h)),
            pl.BlockSpec((4, HEAD_DIM), lambda bi, h, i: (0, 0)),
            pl.BlockSpec((V_DIM, LANES), lambda bi, h, i: (0, 0)),
        ],
        out_specs=pl.BlockSpec((1, tq, V_DIM), lambda bi, h, i: (bi, i, h)),
        out_shape=jax.ShapeDtypeStruct((b, s, D_ATTN), BF16),
        scratch_shapes=[
            pltpu.VMEM((V_ROWS, 2 * tq), F32),
            pltpu.VMEM((1, 2 * tq), F32),
            pltpu.VMEM((KV_TILE, 2 * tq), F32),
            pltpu.VMEM((KV_TILE, 2 * tq), F32),
        ],
        compiler_params=pltpu.CompilerParams(
            dimension_semantics=("arbitrary", "arbitrary", "arbitrary"),
            vmem_limit_bytes=VMEM_LIMIT),
        name="attn",
    )(qlo, qhi, k, vt, ga, lqk, subw)


def _post_kernel(x_ref, a_ref, s_ref, mod_ref, wout_ref, o_ref):
    y = (jnp.dot(a_ref[0], wout_ref[0:D_ATTN, :], preferred_element_type=F32)
         + jnp.dot(s_ref[0], wout_ref[D_ATTN:, :], preferred_element_type=F32))
    gate = mod_ref[0, :, 2 * D_MODEL:3 * D_MODEL]
    o_ref[0] = x_ref[0] + gate * y


def _post(x, a, sg, mod, w_out):
    b, s, _ = x.shape
    ts = ROW_TILE
    return pl.pallas_call(
        _post_kernel,
        grid=(b, s // ts),
        in_specs=[
            pl.BlockSpec((1, ts, D_MODEL), lambda bi, i: (bi, i, 0)),
            pl.BlockSpec((1, ts, D_ATTN), lambda bi, i: (bi, i, 0)),
            pl.BlockSpec((1, ts, D_SGU), lambda bi, i: (bi, i, 0)),
            pl.BlockSpec((1, 1, 3 * D_MODEL), lambda bi, i: (bi, 0, 0)),
            pl.BlockSpec((D_MODEL, D_MODEL), lambda bi, i: (0, 0)),
        ],
        out_specs=pl.BlockSpec((1, ts, D_MODEL), lambda bi, i: (bi, i, 0)),
        out_shape=jax.ShapeDtypeStruct((b, s, D_MODEL), F32),
        compiler_params=pltpu.CompilerParams(
            dimension_semantics=("arbitrary", "arbitrary"), vmem_limit_bytes=VMEM_LIMIT),
        name="post",
    )(x, a, sg, mod, w_out)


def _rope_tables(seq_len):
    half = HEAD_DIM // 2
    inv_freq = 1.0 / (ROPE_THETA ** (jnp.arange(half, dtype=F32) / half))
    ang = jnp.arange(seq_len, dtype=F32)[:, None] * inv_freq[None, :]
    ang = jnp.concatenate([ang, ang, ang, ang], axis=-1)
    cos = jnp.cos(ang)
    sin = jnp.sin(ang)
    first_half = (jnp.arange(LANES) % HEAD_DIM) < half
    sin_lo = jnp.where(first_half[None, :], -sin, 0.0)
    sin_hi = jnp.where(first_half[None, :], 0.0, sin)
    return cos, sin_lo, sin_hi


def _layer(x, mod, p):
    cos, slo, shi = _rope_tables(x.shape[1])
    qlo, qhi, k, vt, ga, sg = _pre(x, mod, p, cos, slo, shi)
    a = _attn(qlo, qhi, k, vt, ga, p["lambda_qk"], p["subln_w"])
    return _post(x, a, sg, mod, p["w_out"])


def kernel(x_prompt, x_sample, c_prompt, c_sample, norm_w, w_ada, b_ada, w_in, w_out, q_norm_w, k_norm_w, lambda_qk, subln_w, sgu_norm_w, sgu_norm_b, w_spatial, b_spatial):
    depth = norm_w.shape[0]
    assert depth == 1
    nb_p = c_prompt.shape[0]
    nb_s = c_sample.shape[0]
    rows = -(-(nb_p + nb_s) // 8) * 8
    c_all = jnp.concatenate(
        [c_prompt, c_sample, jnp.zeros((rows - nb_p - nb_s, D_MODEL), F32)], axis=0)
    y_prompt, y_sample = x_prompt, x_sample
    for l in range(depth):
        p = {
            "norm_w": norm_w[l].reshape(1, D_MODEL),
            "w_in": w_in[l].astype(BF16),
            "w_out": w_out[l].astype(BF16),
            "q_norm_w": jnp.tile(q_norm_w[l], D_ATTN // HEAD_DIM).reshape(1, D_ATTN),
            "k_norm_w": jnp.tile(k_norm_w[l], D_ATTN // HEAD_DIM).reshape(1, D_ATTN),
            "lambda_qk": lambda_qk[l],
            "subln_w": jnp.broadcast_to(subln_w[l][:, None], (V_DIM, LANES)),
            "sgu_norm_w": sgu_norm_w[l].reshape(1, D_SGU),
            "sgu_norm_b": sgu_norm_b[l].reshape(1, D_SGU),
            "w_spatial": w_spatial[l].astype(BF16),
            "b_spatial": jnp.broadcast_to(b_spatial[l][:, :, None], (N_SGU_GROUPS, CHUNK, LANES)),
            "chunk_ones": jnp.kron(jnp.eye(D_ATTN // HEAD_DIM, dtype=F32),
                                   jnp.ones((HEAD_DIM, HEAD_DIM), F32)).astype(BF16),
        }
        mod = _ada(c_all, w_ada[l], b_ada[l])
        mod_p = mod[0:nb_p].reshape(nb_p, 1, 3 * D_MODEL)
        mod_s = mod[nb_p:nb_p + nb_s].reshape(nb_s, 1, 3 * D_MODEL)
        y_prompt = _layer(y_prompt, mod_p, p)
        y_sample = _layer(y_sample, mod_s, p)
    return (y_prompt, y_sample)
```

```python
import functools
import math

import jax
import jax.numpy as jnp
from jax import lax
from jax.experimental import pallas as pl
from jax.experimental.pallas import tpu as pltpu

D_MODEL = 1024
D_ATTN = 512
D_SGU = 512
N_HEADS = 4
HEAD_DIM = 64
V_DIM = 128
N_SGU_GROUPS = 4
CHUNK = 128
ROPE_THETA = 10000.0
EPS = 1e-6
LAMBDA_INIT = 0.8 - 0.6 * math.exp(-0.3 * 0)
LOG2E = 1.4426950408889634

LANES = 128
BF16_SUBLANES = 16
V_ROWS = V_DIM + BF16_SUBLANES
MXU_DEPTH = 256
PRE_ROWS = 512
POST_ROWS = 1024
Q_TILE = 1024
KV_TILE = 512
KV_UNROLL = 4
SAFE_BOUND = 50.0
BOUND_SLACK = 1.05
VMEM_LIMIT = 48 * 1024 * 1024
NEG_BIG = -1e30

F32 = jnp.float32
BF16 = jnp.bfloat16


def _sigmoid(z):
    return 1.0 / (1.0 + jnp.exp(-z))


def _ada_kernel(c_ref, w_ref, b_ref, o_ref):
    c = c_ref[...]
    a = (c * _sigmoid(c)).astype(BF16)
    o_ref[...] = jnp.dot(a, w_ref[...].astype(BF16), preferred_element_type=F32) + b_ref[...]


def _ada(c_all, w_ada, b_ada):
    rows = c_all.shape[0]
    n = w_ada.shape[1]
    tn = 1024
    return pl.pallas_call(
        _ada_kernel,
        grid=(n // tn,),
        in_specs=[
            pl.BlockSpec((rows, D_MODEL), lambda j: (0, 0)),
            pl.BlockSpec((D_MODEL, tn), lambda j: (0, j)),
            pl.BlockSpec((1, tn), lambda j: (0, j)),
        ],
        out_specs=pl.BlockSpec((rows, tn), lambda j: (0, j)),
        out_shape=jax.ShapeDtypeStruct((rows, n), F32),
        name="ada",
    )(c_all, w_ada, b_ada.reshape(1, n))


def _chunk_rms(t, g_ref, w):
    sq = t * t
    hi = sq.astype(BF16)
    lo = (sq - hi.astype(F32)).astype(BF16)
    g = g_ref[...]
    width = g.shape[0]
    ss = jnp.concatenate(
        [jnp.dot(hi[:, c:c + width], g, preferred_element_type=F32)
         + jnp.dot(lo[:, c:c + width], g, preferred_element_type=F32)
         for c in range(0, t.shape[1], width)], axis=1)
    return t * lax.rsqrt(ss * (1.0 / HEAD_DIM) + EPS) * w


def _rope_slab(t, cos, sin_lo, sin_hi):
    fwd = pltpu.roll(t, LANES - HEAD_DIM // 2, 1)
    bwd = pltpu.roll(t, HEAD_DIM // 2, 1)
    return t * cos + fwd * sin_lo + bwd * sin_hi


def _pre_kernel(x_ref, mod_ref, nw_ref, win_ref, qnw_ref, knw_ref, cos_ref, slo_ref, shi_ref,
                g_ref, sgw_ref, sgb_ref, wsp_ref, bsp_ref,
                qlo_ref, qhi_ref, k_ref, vt_ref, ga_ref, s_ref):
    ts = x_ref.shape[1]
    x = x_ref[0]
    shift = mod_ref[0, :, 0:D_MODEL]
    scale = mod_ref[0, :, D_MODEL:2 * D_MODEL]
    xn = x * lax.rsqrt(jnp.mean(x * x, axis=-1, keepdims=True) + EPS) * nw_ref[...]
    hb = (xn * (1.0 + scale) + shift).astype(BF16)

    def proj(idx):
        return jnp.dot(hb, win_ref[:, idx * 512:(idx + 1) * 512], preferred_element_type=F32)

    cos = cos_ref[...]
    slo = slo_ref[...]
    shi = shi_ref[...]
    lane = lax.broadcasted_iota(jnp.int32, (ts, LANES), 1)
    first_map = lane < HEAD_DIM

    qn = _chunk_rms(proj(0), g_ref, qnw_ref[...])
    for h in range(N_HEADS):
        qr = _rope_slab(qn[:, h * LANES:(h + 1) * LANES], cos, slo, shi) * (HEAD_DIM ** -0.5 * LOG2E)
        qlo_ref[0, h] = jnp.where(first_map, qr, 0.0).astype(BF16)
        qhi_ref[0, h] = jnp.where(first_map, 0.0, qr).astype(BF16)

    kn = _chunk_rms(proj(1), g_ref, knw_ref[...])
    for h in range(N_HEADS):
        k_ref[0, h] = _rope_slab(kn[:, h * LANES:(h + 1) * LANES], cos, slo, shi).astype(BF16)

    v = proj(2)
    for h in range(N_HEADS):
        vt_ref[0, h, 0:V_DIM, :] = v[:, h * LANES:(h + 1) * LANES].T.astype(BF16)
        vt_ref[0, h, V_DIM:V_ROWS, :] = jnp.ones((V_ROWS - V_DIM, ts), BF16)

    za = proj(3)
    ga_ref[0] = (za * _sigmoid(za)).astype(BF16)

    u = proj(4)
    vg = proj(5)
    zs = proj(6)
    mu = jnp.mean(vg, axis=-1, keepdims=True)
    vc = vg - mu
    var = jnp.mean(vc * vc, axis=-1, keepdims=True)
    vn = (vc * lax.rsqrt(var + EPS) * sgw_ref[...] + sgb_ref[...]).astype(BF16)
    gate = u * (zs * _sigmoid(zs))
    nc = ts // CHUNK
    for g in range(N_SGU_GROUPS):
        gs = slice(g * LANES, (g + 1) * LANES)
        xg = jnp.concatenate([vn[c * CHUNK:(c + 1) * CHUNK, gs] for c in range(nc)], axis=1)
        mg = jnp.dot(wsp_ref[g], xg, preferred_element_type=F32)
        bias = bsp_ref[g]
        for c in range(nc):
            rs = slice(c * CHUNK, (c + 1) * CHUNK)
            s_ref[0, rs, gs] = (gate[rs, gs] * (mg[:, c * LANES:(c + 1) * LANES] + bias)).astype(BF16)


def _pre(x, mod, p, cos, slo, shi):
    b, s, _ = x.shape
    ts = PRE_ROWS
    full = lambda shape: pl.BlockSpec(shape, lambda bi, i: (0,) * len(shape))
    head_out = lambda: pl.BlockSpec((1, N_HEADS, ts, LANES), lambda bi, i: (bi, 0, i, 0))
    return pl.pallas_call(
        _pre_kernel,
        grid=(b, s // ts),
        in_specs=[
            pl.BlockSpec((1, ts, D_MODEL), lambda bi, i: (bi, i, 0)),
            pl.BlockSpec((1, 1, 3 * D_MODEL), lambda bi, i: (bi, 0, 0)),
            full((1, D_MODEL)),
            full((D_MODEL, 7 * 512)),
            full((1, D_ATTN)),
            full((1, D_ATTN)),
            pl.BlockSpec((ts, LANES), lambda bi, i: (i, 0)),
            pl.BlockSpec((ts, LANES), lambda bi, i: (i, 0)),
            pl.BlockSpec((ts, LANES), lambda bi, i: (i, 0)),
            full((MXU_DEPTH, MXU_DEPTH)),
            full((1, D_SGU)),
            full((1, D_SGU)),
            full((N_SGU_GROUPS, CHUNK, CHUNK)),
            full((N_SGU_GROUPS, CHUNK, LANES)),
        ],
        out_specs=[
            head_out(),
            head_out(),
            head_out(),
            pl.BlockSpec((1, N_HEADS, V_ROWS, ts), lambda bi, i: (bi, 0, 0, i)),
            pl.BlockSpec((1, ts, D_ATTN), lambda bi, i: (bi, i, 0)),
            pl.BlockSpec((1, ts, D_SGU), lambda bi, i: (bi, i, 0)),
        ],
        out_shape=[
            jax.ShapeDtypeStruct((b, N_HEADS, s, LANES), BF16),
            jax.ShapeDtypeStruct((b, N_HEADS, s, LANES), BF16),
            jax.ShapeDtypeStruct((b, N_HEADS, s, LANES), BF16),
            jax.ShapeDtypeStruct((b, N_HEADS, V_ROWS, s), BF16),
            jax.ShapeDtypeStruct((b, s, D_ATTN), BF16),
            jax.ShapeDtypeStruct((b, s, D_SGU), BF16),
        ],
        compiler_params=pltpu.CompilerParams(
            dimension_semantics=("arbitrary", "arbitrary"), vmem_limit_bytes=VMEM_LIMIT),
        name="pre",
    )(x, mod, p["norm_w"], p["w_in"], p["q_norm_w"], p["k_norm_w"], cos, slo, shi,
      p["chunk_ones"], p["sgu_norm_w"], p["sgu_norm_b"], p["w_spatial"], p["b_spatial"])


def _attn_kernel(qlo_ref, qhi_ref, k_ref, vt_ref, ga_ref, lqk_ref, subw_ref, o_ref,
                 acc_ref, kn_ref):
    tq = qlo_ref.shape[2]
    s_len = k_ref.shape[2]
    tk = KV_TILE
    n_blocks = s_len // tk
    contract_lanes = (((1,), (1,)), ((), ()))
    qd = jnp.concatenate([qlo_ref[0, 0], qhi_ref[0, 0]], axis=0)

    def scores(blk):
        off = pl.multiple_of(blk * tk, tk)
        kb = k_ref[0, 0, pl.ds(off, tk), :]
        return lax.dot_general(kb, qd, contract_lanes, preferred_element_type=F32)

    @pl.when(pl.program_id(2) == 0)
    def _():
        lane = lax.broadcasted_iota(jnp.int32, (8, LANES), 1)
        row = lax.broadcasted_iota(jnp.int32, (8, LANES), 0)
        sel = jnp.where((lane >= HEAD_DIM) == (row == 1), 1.0, 0.0).astype(BF16)

        def kbody(c, mx):
            kb = k_ref[0, 0, pl.ds(pl.multiple_of(c * tk, tk), tk), :].astype(F32)
            ksq = (kb * kb).astype(BF16)
            return jnp.maximum(mx, lax.dot_general(sel, ksq, contract_lanes,
                                                   preferred_element_type=F32))

        mx = lax.fori_loop(0, n_blocks, kbody, jnp.zeros((8, tk), F32))
        kn_ref[...] = jnp.broadcast_to(jnp.max(mx, axis=1, keepdims=True), (8, LANES))

    qf = qd.astype(F32)
    qn2 = lax.dot_general(jnp.ones((8, LANES), BF16), (qf * qf).astype(BF16), contract_lanes,
                          preferred_element_type=F32)[0:1]
    kn2 = jnp.concatenate([jnp.broadcast_to(kn_ref[0:1, 0:1], (1, tq)),
                           jnp.broadcast_to(kn_ref[1:2, 0:1], (1, tq))], axis=1)
    bound = jnp.sqrt(qn2 * kn2) * BOUND_SLACK

    def exact_row_max():
        def mbody(j, m):
            return jnp.maximum(m, jnp.max(scores(j), axis=0, keepdims=True))
        return lax.fori_loop(0, n_blocks, mbody, jnp.full((1, 2 * tq), NEG_BIG, F32))

    offset = lax.cond(jnp.max(bound) <= SAFE_BOUND, lambda: bound, exact_row_max)

    acc_ref[...] = jnp.zeros(acc_ref.shape, F32)

    def body(it, carry):
        pv = None
        for u in range(KV_UNROLL):
            blk = it * KV_UNROLL + u
            p = jnp.exp2(scores(blk) - offset).astype(BF16)
            vtb = vt_ref[0, 0, :, pl.ds(pl.multiple_of(blk * tk, tk), tk)]
            d = jnp.dot(vtb, p, preferred_element_type=F32)
            pv = d if pv is None else pv + d
        acc_ref[...] += pv
        return carry

    lax.fori_loop(0, n_blocks // KV_UNROLL, body, 0)

    lq = lqk_ref[...]
    lam = (jnp.exp(jnp.sum(lq[0:1] * lq[1:2], axis=1, keepdims=True))
           - jnp.exp(jnp.sum(lq[2:3] * lq[3:4], axis=1, keepdims=True)) + LAMBDA_INIT)
    acc = acc_ref[...]
    on_all = acc[0:V_DIM] * (1.0 / acc[V_DIM:V_DIM + 1])
    o = on_all[:, :tq] - lam * on_all[:, tq:]
    ms = jnp.mean(o * o, axis=0, keepdims=True)
    subw = jnp.concatenate([subw_ref[...]] * (tq // LANES), axis=1)
    on = o * lax.rsqrt(ms + EPS) * subw * (1.0 - LAMBDA_INIT)
    o_ref[0] = (on.T * ga_ref[0].astype(F32)).astype(BF16)


def _attn(qlo, qhi, k, vt, ga, lqk, subw):
    b, _, s, _ = k.shape
    tq = Q_TILE
    q_spec = lambda: pl.BlockSpec((1, 1, tq, LANES), lambda bi, h, i: (bi, h, i, 0))
    return pl.pallas_call(
        _attn_kernel,
        grid=(b, N_HEADS, s // tq),
        in_specs=[
            q_spec(),
            q_spec(),
            pl.BlockSpec((1, 1, s, LANES), lambda bi, h, i: (bi, h, 0, 0)),
            pl.BlockSpec((1, 1, V_ROWS, s), lambda bi, h, i: (bi, h, 0, 0)),
            pl.BlockSpec((1, tq, V_DIM), lambda bi, h, i: (bi, i, h)),
            pl.BlockSpec((4, HEAD_DIM), lambda bi, h, i: (0, 0)),
            pl.BlockSpec((V_DIM, LANES), lambda bi, h, i: (0, 0)),
        ],
        out_specs=pl.BlockSpec((1, tq, V_DIM), lambda bi, h, i: (bi, i, h)),
        out_shape=jax.ShapeDtypeStruct((b, s, D_ATTN), BF16),
        scratch_shapes=[
            pltpu.VMEM((V_ROWS, 2 * tq), F32),
            pltpu.VMEM((8, LANES), F32),
        ],
        compiler_params=pltpu.CompilerParams(
            dimension_semantics=("arbitrary", "arbitrary", "arbitrary"),
            vmem_limit_bytes=VMEM_LIMIT),
        name="attn",
    )(qlo, qhi, k, vt, ga, lqk, subw)


def _post_kernel(x_ref, a_ref, s_ref, mod_ref, wout_ref, o_ref):
    y = (jnp.dot(a_ref[0], wout_ref[0:D_ATTN, :], preferred_element_type=F32)
         + jnp.dot(s_ref[0], wout_ref[D_ATTN:, :], preferred_element_type=F32))
    gate = mod_ref[0, :, 2 * D_MODEL:3 * D_MODEL]
    o_ref[0] = x_ref[0] + gate * y


def _post(x, a, sg, mod, w_out):
    b, s, _ = x.shape
    ts = POST_ROWS
    return pl.pallas_call(
        _post_kernel,
        grid=(b, s // ts),
        in_specs=[
            pl.BlockSpec((1, ts, D_MODEL), lambda bi, i: (bi, i, 0)),
            pl.BlockSpec((1, ts, D_ATTN), lambda bi, i: (bi, i, 0)),
            pl.BlockSpec((1, ts, D_SGU), lambda bi, i: (bi, i, 0)),
            pl.BlockSpec((1, 1, 3 * D_MODEL), lambda bi, i: (bi, 0, 0)),
            pl.BlockSpec((D_MODEL, D_MODEL), lambda bi, i: (0, 0)),
        ],
        out_specs=pl.BlockSpec((1, ts, D_MODEL), lambda bi, i: (bi, i, 0)),
        out_shape=jax.ShapeDtypeStruct((b, s, D_MODEL), F32),
        compiler_params=pltpu.CompilerParams(
            dimension_semantics=("arbitrary", "arbitrary"), vmem_limit_bytes=VMEM_LIMIT),
        name="post",
    )(x, a, sg, mod, w_out)


def _rope_tables(seq_len):
    half = HEAD_DIM // 2
    inv_freq = 1.0 / (ROPE_THETA ** (jnp.arange(half, dtype=F32) / half))
    ang = jnp.arange(seq_len, dtype=F32)[:, None] * inv_freq[None, :]
    ang = jnp.concatenate([ang, ang, ang, ang], axis=-1)
    cos = jnp.cos(ang)
    sin = jnp.sin(ang)
    first_half = (jnp.arange(LANES) % HEAD_DIM) < half
    sin_lo = jnp.where(first_half[None, :], -sin, 0.0)
    sin_hi = jnp.where(first_half[None, :], 0.0, sin)
    return cos, sin_lo, sin_hi


def _layer(x, mod, p, rope):
    qlo, qhi, k, vt, ga, sg = _pre(x, mod, p, *rope)
    a = _attn(qlo, qhi, k, vt, ga, p["lambda_qk"], p["subln_w"])
    return _post(x, a, sg, mod, p["w_out"])


def kernel(x_prompt, x_sample, c_prompt, c_sample, norm_w, w_ada, b_ada, w_in, w_out, q_norm_w, k_norm_w, lambda_qk, subln_w, sgu_norm_w, sgu_norm_b, w_spatial, b_spatial):
    depth = norm_w.shape[0]
    assert depth == 1
    nb_p = c_prompt.shape[0]
    nb_s = c_sample.shape[0]
    rows = -(-(nb_p + nb_s) // 8) * 8
    c_all = jnp.concatenate(
        [c_prompt, c_sample, jnp.zeros((rows - nb_p - nb_s, D_MODEL), F32)], axis=0)
    y_prompt, y_sample = x_prompt, x_sample
    for l in range(depth):
        p = {
            "norm_w": norm_w[l].reshape(1, D_MODEL),
            "w_in": w_in[l].astype(BF16),
            "w_out": w_out[l].astype(BF16),
            "q_norm_w": jnp.tile(q_norm_w[l], D_ATTN // HEAD_DIM).reshape(1, D_ATTN),
            "k_norm_w": jnp.tile(k_norm_w[l], D_ATTN // HEAD_DIM).reshape(1, D_ATTN),
            "lambda_qk": lambda_qk[l],
            "subln_w": jnp.broadcast_to(subln_w[l][:, None], (V_DIM, LANES)),
            "sgu_norm_w": sgu_norm_w[l].reshape(1, D_SGU),
            "sgu_norm_b": sgu_norm_b[l].reshape(1, D_SGU),
            "w_spatial": w_spatial[l].astype(BF16),
            "b_spatial": jnp.broadcast_to(b_spatial[l][:, :, None], (N_SGU_GROUPS, CHUNK, LANES)),
            "chunk_ones": jnp.kron(jnp.eye(MXU_DEPTH // HEAD_DIM, dtype=F32),
                                   jnp.ones((HEAD_DIM, HEAD_DIM), F32)).astype(BF16),
        }
        mod = _ada(c_all, w_ada[l], b_ada[l])
        mod_p = mod[0:nb_p].reshape(nb_p, 1, 3 * D_MODEL)
        mod_s = mod[nb_p:nb_p + nb_s].reshape(nb_s, 1, 3 * D_MODEL)
        rope = _rope_tables(max(x_prompt.shape[1], x_sample.shape[1]))
        y_prompt = _layer(y_prompt, mod_p, p, rope)
        y_sample = _layer(y_sample, mod_s, p, rope)
    return (y_prompt, y_sample)
```

```python
import functools
import math

import jax
import jax.numpy as jnp
from jax import lax
from jax.experimental import pallas as pl
from jax.experimental.pallas import tpu as pltpu

D_MODEL = 1024
D_ATTN = 512
D_SGU = 512
N_HEADS = 4
HEAD_DIM = 64
V_DIM = 128
N_SGU_GROUPS = 4
CHUNK = 128
ROPE_THETA = 10000.0
EPS = 1e-6
LAMBDA_INIT = 0.8 - 0.6 * math.exp(-0.3 * 0)
LOG2E = 1.4426950408889634

LANES = 128
BF16_SUBLANES = 16
V_ROWS = V_DIM + BF16_SUBLANES
MXU_DEPTH = 256
PRE_ROWS = 1024
POST_ROWS = 1024
Q_TILE = 1024
KV_TILE = 512
KV_UNROLL = 8
SAFE_BOUND = 50.0
BOUND_SLACK = 1.05
VMEM_LIMIT = 48 * 1024 * 1024
NEG_BIG = -1e30

F32 = jnp.float32
BF16 = jnp.bfloat16


def _sigmoid(z):
    return 1.0 / (1.0 + jnp.exp(-z))


def _ada_kernel(c_ref, w_ref, b_ref, o_ref):
    c = c_ref[...]
    a = (c * _sigmoid(c)).astype(BF16)
    o_ref[...] = jnp.dot(a, w_ref[...].astype(BF16), preferred_element_type=F32) + b_ref[...]


def _ada(c_all, w_ada, b_ada):
    rows = c_all.shape[0]
    n = w_ada.shape[1]
    tn = 1024
    return pl.pallas_call(
        _ada_kernel,
        grid=(n // tn,),
        in_specs=[
            pl.BlockSpec((rows, D_MODEL), lambda j: (0, 0)),
            pl.BlockSpec((D_MODEL, tn), lambda j: (0, j)),
            pl.BlockSpec((1, tn), lambda j: (0, j)),
        ],
        out_specs=pl.BlockSpec((rows, tn), lambda j: (0, j)),
        out_shape=jax.ShapeDtypeStruct((rows, n), F32),
        name="ada",
    )(c_all, w_ada, b_ada.reshape(1, n))


def _chunk_rms(t, g_ref, w):
    sq = t * t
    hi = sq.astype(BF16)
    lo = (sq - hi.astype(F32)).astype(BF16)
    g = g_ref[...]
    width = g.shape[0]
    ss = jnp.concatenate(
        [jnp.dot(hi[:, c:c + width], g, preferred_element_type=F32)
         + jnp.dot(lo[:, c:c + width], g, preferred_element_type=F32)
         for c in range(0, t.shape[1], width)], axis=1)
    return t * lax.rsqrt(ss * (1.0 / HEAD_DIM) + EPS) * w


def _rope_slab(t, cos, sin_lo, sin_hi):
    fwd = pltpu.roll(t, LANES - HEAD_DIM // 2, 1)
    bwd = pltpu.roll(t, HEAD_DIM // 2, 1)
    return t * cos + fwd * sin_lo + bwd * sin_hi


def _pre_kernel(x_ref, mod_ref, nw_ref, win_ref, qnw_ref, knw_ref, cos_ref, slo_ref, shi_ref,
                g_ref, sgw_ref, sgb_ref, wsp_ref, bsp_ref,
                qlo_ref, qhi_ref, k_ref, vt_ref, ga_ref, s_ref):
    ts = x_ref.shape[1]
    x = x_ref[0]
    shift = mod_ref[0, :, 0:D_MODEL]
    scale = mod_ref[0, :, D_MODEL:2 * D_MODEL]
    xn = x * lax.rsqrt(jnp.mean(x * x, axis=-1, keepdims=True) + EPS) * nw_ref[...]
    hb = (xn * (1.0 + scale) + shift).astype(BF16)

    def proj(idx):
        return jnp.dot(hb, win_ref[:, idx * 512:(idx + 1) * 512], preferred_element_type=F32)

    cos = cos_ref[...]
    slo = slo_ref[...]
    shi = shi_ref[...]
    first_map = lax.broadcasted_iota(jnp.int32, (LANES, ts), 0) < HEAD_DIM

    qn = _chunk_rms(proj(0), g_ref, qnw_ref[...])
    for h in range(N_HEADS):
        qr = _rope_slab(qn[:, h * LANES:(h + 1) * LANES], cos, slo, shi) * (HEAD_DIM ** -0.5 * LOG2E)
        qrt = qr.T
        qlo_ref[0, h] = jnp.where(first_map, qrt, 0.0).astype(BF16)
        qhi_ref[0, h] = jnp.where(first_map, 0.0, qrt).astype(BF16)

    kn = _chunk_rms(proj(1), g_ref, knw_ref[...])
    for h in range(N_HEADS):
        k_ref[0, h] = _rope_slab(kn[:, h * LANES:(h + 1) * LANES], cos, slo, shi).astype(BF16)

    v = proj(2)
    for h in range(N_HEADS):
        vt_ref[0, h, 0:V_DIM, :] = v[:, h * LANES:(h + 1) * LANES].T.astype(BF16)
        vt_ref[0, h, V_DIM:V_ROWS, :] = jnp.ones((V_ROWS - V_DIM, ts), BF16)

    za = proj(3)
    ga_ref[0] = (za * _sigmoid(za)).astype(BF16)

    u = proj(4)
    vg = proj(5)
    zs = proj(6)
    mu = jnp.mean(vg, axis=-1, keepdims=True)
    vc = vg - mu
    var = jnp.mean(vc * vc, axis=-1, keepdims=True)
    vn = (vc * lax.rsqrt(var + EPS) * sgw_ref[...] + sgb_ref[...]).astype(BF16)
    gate = u * (zs * _sigmoid(zs))
    nc = ts // CHUNK
    for g in range(N_SGU_GROUPS):
        gs = slice(g * LANES, (g + 1) * LANES)
        xg = jnp.concatenate([vn[c * CHUNK:(c + 1) * CHUNK, gs] for c in range(nc)], axis=1)
        mg = jnp.dot(wsp_ref[g], xg, preferred_element_type=F32)
        bias = bsp_ref[g]
        for c in range(nc):
            rs = slice(c * CHUNK, (c + 1) * CHUNK)
            s_ref[0, rs, gs] = (gate[rs, gs] * (mg[:, c * LANES:(c + 1) * LANES] + bias)).astype(BF16)


def _pre(x, mod, p, cos, slo, shi):
    b, s, _ = x.shape
    ts = PRE_ROWS
    full = lambda shape: pl.BlockSpec(shape, lambda bi, i: (0,) * len(shape))
    head_out = lambda: pl.BlockSpec((1, N_HEADS, ts, LANES), lambda bi, i: (bi, 0, i, 0))
    return pl.pallas_call(
        _pre_kernel,
        grid=(b, s // ts),
        in_specs=[
            pl.BlockSpec((1, ts, D_MODEL), lambda bi, i: (bi, i, 0)),
            pl.BlockSpec((1, 1, 3 * D_MODEL), lambda bi, i: (bi, 0, 0)),
            full((1, D_MODEL)),
            full((D_MODEL, 7 * 512)),
            full((1, D_ATTN)),
            full((1, D_ATTN)),
            pl.BlockSpec((ts, LANES), lambda bi, i: (i, 0)),
            pl.BlockSpec((ts, LANES), lambda bi, i: (i, 0)),
            pl.BlockSpec((ts, LANES), lambda bi, i: (i, 0)),
            full((MXU_DEPTH, MXU_DEPTH)),
            full((1, D_SGU)),
            full((1, D_SGU)),
            full((N_SGU_GROUPS, CHUNK, CHUNK)),
            full((N_SGU_GROUPS, CHUNK, LANES)),
        ],
        out_specs=[
            pl.BlockSpec((1, N_HEADS, LANES, ts), lambda bi, i: (bi, 0, 0, i)),
            pl.BlockSpec((1, N_HEADS, LANES, ts), lambda bi, i: (bi, 0, 0, i)),
            head_out(),
            pl.BlockSpec((1, N_HEADS, V_ROWS, ts), lambda bi, i: (bi, 0, 0, i)),
            pl.BlockSpec((1, ts, D_ATTN), lambda bi, i: (bi, i, 0)),
            pl.BlockSpec((1, ts, D_SGU), lambda bi, i: (bi, i, 0)),
        ],
        out_shape=[
            jax.ShapeDtypeStruct((b, N_HEADS, LANES, s), BF16),
            jax.ShapeDtypeStruct((b, N_HEADS, LANES, s), BF16),
            jax.ShapeDtypeStruct((b, N_HEADS, s, LANES), BF16),
            jax.ShapeDtypeStruct((b, N_HEADS, V_ROWS, s), BF16),
            jax.ShapeDtypeStruct((b, s, D_ATTN), BF16),
            jax.ShapeDtypeStruct((b, s, D_SGU), BF16),
        ],
        compiler_params=pltpu.CompilerParams(
            dimension_semantics=("arbitrary", "arbitrary"), vmem_limit_bytes=VMEM_LIMIT),
        name="pre",
    )(x, mod, p["norm_w"], p["w_in"], p["q_norm_w"], p["k_norm_w"], cos, slo, shi,
      p["chunk_ones"], p["sgu_norm_w"], p["sgu_norm_b"], p["w_spatial"], p["b_spatial"])


def _attn_kernel(bound_ref, qlo_ref, qhi_ref, k_ref, vt_ref, ga_ref, lqk_ref, subw_ref, o_ref,
                 acc_ref):
    tq = qlo_ref.shape[3]
    s_len = k_ref.shape[2]
    tk = KV_TILE
    n_blocks = s_len // tk
    qdt = jnp.concatenate([qlo_ref[0, 0], qhi_ref[0, 0]], axis=1)

    def scores(blk):
        off = pl.multiple_of(blk * tk, tk)
        kb = k_ref[0, 0, pl.ds(off, tk), :]
        return jnp.dot(kb, qdt, preferred_element_type=F32)

    bound = bound_ref[0]

    def exact_row_max():
        def mbody(j, m):
            return jnp.maximum(m, jnp.max(scores(j), axis=0, keepdims=True))
        return lax.fori_loop(0, n_blocks, mbody, jnp.full((1, 2 * tq), NEG_BIG, F32))

    offset = lax.cond(bound <= SAFE_BOUND, lambda: jnp.full((1, 2 * tq), bound, F32), exact_row_max)

    acc_ref[...] = jnp.zeros(acc_ref.shape, F32)

    def body(it, carry):
        pv = None
        for u in range(KV_UNROLL):
            blk = it * KV_UNROLL + u
            p = jnp.exp2(scores(blk) - offset).astype(BF16)
            vtb = vt_ref[0, 0, :, pl.ds(pl.multiple_of(blk * tk, tk), tk)]
            d = jnp.dot(vtb, p, preferred_element_type=F32)
            pv = d if pv is None else pv + d
        acc_ref[...] += pv
        return carry

    lax.fori_loop(0, n_blocks // KV_UNROLL, body, 0)

    lq = lqk_ref[...]
    lam = (jnp.exp(jnp.sum(lq[0:1] * lq[1:2], axis=1, keepdims=True))
           - jnp.exp(jnp.sum(lq[2:3] * lq[3:4], axis=1, keepdims=True)) + LAMBDA_INIT)
    acc = acc_ref[...]
    on_all = acc[0:V_DIM] * (1.0 / acc[V_DIM:V_DIM + 1])
    o = on_all[:, :tq] - lam * on_all[:, tq:]
    ms = jnp.mean(o * o, axis=0, keepdims=True)
    subw = jnp.concatenate([subw_ref[...]] * (tq // LANES), axis=1)
    on = o * lax.rsqrt(ms + EPS) * subw * (1.0 - LAMBDA_INIT)
    o_ref[0] = (on.T * ga_ref[0].astype(F32)).astype(BF16)


def _score_bound(q_norm_w, k_norm_w):
    bound = (HEAD_DIM ** 0.5 * LOG2E * BOUND_SLACK) * jnp.max(jnp.abs(q_norm_w)) * jnp.max(jnp.abs(k_norm_w))
    return bound.reshape(1).astype(F32)


def _attn(bound, qlo, qhi, k, vt, ga, lqk, subw):
    b, _, s, _ = k.shape
    tq = Q_TILE
    q_spec = lambda: pl.BlockSpec((1, 1, LANES, tq), lambda bi, h, i: (bi, h, 0, i))
    return pl.pallas_call(
        _attn_kernel,
        grid=(b, N_HEADS, s // tq),
        in_specs=[
            pl.BlockSpec(memory_space=pltpu.SMEM),
            q_spec(),
            q_spec(),
            pl.BlockSpec((1, 1, s, LANES), lambda bi, h, i: (bi, h, 0, 0)),
            pl.BlockSpec((1, 1, V_ROWS, s), lambda bi, h, i: (bi, h, 0, 0)),
            pl.BlockSpec((1, tq, V_DIM), lambda bi, h, i: (bi, i, h)),
            pl.BlockSpec((4, HEAD_DIM), lambda bi, h, i: (0, 0)),
            pl.BlockSpec((V_DIM, LANES), lambda bi, h, i: (0, 0)),
        ],
        out_specs=pl.BlockSpec((1, tq, V_DIM), lambda bi, h, i: (bi, i, h)),
        out_shape=jax.ShapeDtypeStruct((b, s, D_ATTN), BF16),
        scratch_shapes=[
            pltpu.VMEM((V_ROWS, 2 * tq), F32),
        ],
        compiler_params=pltpu.CompilerParams(
            dimension_semantics=("arbitrary", "arbitrary", "arbitrary"),
            vmem_limit_bytes=VMEM_LIMIT),
        name="attn",
    )(bound, qlo, qhi, k, vt, ga, lqk, subw)


def _post_kernel(x_ref, a_ref, s_ref, mod_ref, wout_ref, o_ref):
    y = (jnp.dot(a_ref[0], wout_ref[0:D_ATTN, :], preferred_element_type=F32)
         + jnp.dot(s_ref[0], wout_ref[D_ATTN:, :], preferred_element_type=F32))
    gate = mod_ref[0, :, 2 * D_MODEL:3 * D_MODEL]
    o_ref[0] = x_ref[0] + gate * y


def _post(x, a, sg, mod, w_out):
    b, s, _ = x.shape
    ts = POST_ROWS
    return pl.pallas_call(
        _post_kernel,
        grid=(b, s // ts),
        in_specs=[
            pl.BlockSpec((1, ts, D_MODEL), lambda bi, i: (bi, i, 0)),
            pl.BlockSpec((1, ts, D_ATTN), lambda bi, i: (bi, i, 0)),
            pl.BlockSpec((1, ts, D_SGU), lambda bi, i: (bi, i, 0)),
            pl.BlockSpec((1, 1, 3 * D_MODEL), lambda bi, i: (bi, 0, 0)),
            pl.BlockSpec((D_MODEL, D_MODEL), lambda bi, i: (0, 0)),
        ],
        out_specs=pl.BlockSpec((1, ts, D_MODEL), lambda bi, i: (bi, i, 0)),
        out_shape=jax.ShapeDtypeStruct((b, s, D_MODEL), F32),
        compiler_params=pltpu.CompilerParams(
            dimension_semantics=("arbitrary", "arbitrary"), vmem_limit_bytes=VMEM_LIMIT),
        name="post",
    )(x, a, sg, mod, w_out)


def _rope_tables(seq_len):
    half = HEAD_DIM // 2
    inv_freq = 1.0 / (ROPE_THETA ** (jnp.arange(half, dtype=F32) / half))
    ang = jnp.arange(seq_len, dtype=F32)[:, None] * inv_freq[None, :]
    ang = jnp.concatenate([ang, ang, ang, ang], axis=-1)
    cos = jnp.cos(ang)
    sin = jnp.sin(ang)
    first_half = (jnp.arange(LANES) % HEAD_DIM) < half
    sin_lo = jnp.where(first_half[None, :], -sin, 0.0)
    sin_hi = jnp.where(first_half[None, :], 0.0, sin)
    return cos, sin_lo, sin_hi


def _layer(x, mod, p, rope):
    qlo, qhi, k, vt, ga, sg = _pre(x, mod, p, *rope)
    a = _attn(p["score_bound"], qlo, qhi, k, vt, ga, p["lambda_qk"], p["subln_w"])
    return _post(x, a, sg, mod, p["w_out"])


def kernel(x_prompt, x_sample, c_prompt, c_sample, norm_w, w_ada, b_ada, w_in, w_out, q_norm_w, k_norm_w, lambda_qk, subln_w, sgu_norm_w, sgu_norm_b, w_spatial, b_spatial):
    depth = norm_w.shape[0]
    assert depth == 1
    nb_p = c_prompt.shape[0]
    nb_s = c_sample.shape[0]
    rows = -(-(nb_p + nb_s) // 8) * 8
    c_all = jnp.concatenate(
        [c_prompt, c_sample, jnp.zeros((rows - nb_p - nb_s, D_MODEL), F32)], axis=0)
    y_prompt, y_sample = x_prompt, x_sample
    for l in range(depth):
        p = {
            "norm_w": norm_w[l].reshape(1, D_MODEL),
            "w_in": w_in[l].astype(BF16),
            "w_out": w_out[l].astype(BF16),
            "q_norm_w": jnp.tile(q_norm_w[l], D_ATTN // HEAD_DIM).reshape(1, D_ATTN),
            "k_norm_w": jnp.tile(k_norm_w[l], D_ATTN // HEAD_DIM).reshape(1, D_ATTN),
            "lambda_qk": lambda_qk[l],
            "score_bound": _score_bound(q_norm_w[l], k_norm_w[l]),
            "subln_w": jnp.broadcast_to(subln_w[l][:, None], (V_DIM, LANES)),
            "sgu_norm_w": sgu_norm_w[l].reshape(1, D_SGU),
            "sgu_norm_b": sgu_norm_b[l].reshape(1, D_SGU),
            "w_spatial": w_spatial[l].astype(BF16),
            "b_spatial": jnp.broadcast_to(b_spatial[l][:, :, None], (N_SGU_GROUPS, CHUNK, LANES)),
            "chunk_ones": jnp.kron(jnp.eye(MXU_DEPTH // HEAD_DIM, dtype=F32),
                                   jnp.ones((HEAD_DIM, HEAD_DIM), F32)).astype(BF16),
        }
        mod = _ada(c_all, w_ada[l], b_ada[l])
        mod_p = mod[0:nb_p].reshape(nb_p, 1, 3 * D_MODEL)
        mod_s = mod[nb_p:nb_p + nb_s].reshape(nb_s, 1, 3 * D_MODEL)
        rope = _rope_tables(max(x_prompt.shape[1], x_sample.shape[1]))
        y_prompt = _layer(y_prompt, mod_p, p, rope)
        y_sample = _layer(y_sample, mod_s, p, rope)
    return (y_prompt, y_sample)
```

```python
import functools
import math

import jax
import jax.numpy as jnp
from jax import lax
from jax.experimental import pallas as pl
from jax.experimental.pallas import tpu as pltpu

D_MODEL = 1024
D_ATTN = 512
D_SGU = 512
N_HEADS = 4
HEAD_DIM = 64
V_DIM = 128
N_SGU_GROUPS = 4
CHUNK = 128
ROPE_THETA = 10000.0
EPS = 1e-6
LAMBDA_INIT = 0.8 - 0.6 * math.exp(-0.3 * 0)
LOG2E = 1.4426950408889634

LANES = 128
BF16_SUBLANES = 16
V_ROWS = V_DIM + BF16_SUBLANES
MXU_DEPTH = 256
PRE_ROWS = 1024
POST_ROWS = 1024
Q_TILE = 1024
KV_TILE = 512
KV_UNROLL = 8
SAFE_BOUND = 50.0
BOUND_SLACK = 1.05
VMEM_LIMIT = 48 * 1024 * 1024
NEG_BIG = -1e30

F32 = jnp.float32
BF16 = jnp.bfloat16


def _sigmoid(z):
    return 1.0 / (1.0 + jnp.exp(-z))


def _ada_kernel(c_ref, w_ref, b_ref, o_ref):
    c = c_ref[...]
    a = (c * _sigmoid(c)).astype(BF16)
    o_ref[...] = jnp.dot(a, w_ref[...].astype(BF16), preferred_element_type=F32) + b_ref[...]


def _ada(c_all, w_ada, b_ada):
    rows = c_all.shape[0]
    n = w_ada.shape[1]
    tn = 1024
    return pl.pallas_call(
        _ada_kernel,
        grid=(n // tn,),
        in_specs=[
            pl.BlockSpec((rows, D_MODEL), lambda j: (0, 0)),
            pl.BlockSpec((D_MODEL, tn), lambda j: (0, j)),
            pl.BlockSpec((1, tn), lambda j: (0, j)),
        ],
        out_specs=pl.BlockSpec((rows, tn), lambda j: (0, j)),
        out_shape=jax.ShapeDtypeStruct((rows, n), F32),
        name="ada",
    )(c_all, w_ada, b_ada.reshape(1, n))


def _chunk_rms(t, g_ref, w):
    sq = t * t
    hi = sq.astype(BF16)
    lo = (sq - hi.astype(F32)).astype(BF16)
    g = g_ref[...]
    width = g.shape[0]
    ss = jnp.concatenate(
        [jnp.dot(hi[:, c:c + width], g, preferred_element_type=F32)
         + jnp.dot(lo[:, c:c + width], g, preferred_element_type=F32)
         for c in range(0, t.shape[1], width)], axis=1)
    return t * lax.rsqrt(ss * (1.0 / HEAD_DIM) + EPS) * w


def _rope_slab(t, cos, sin_lo, sin_hi):
    fwd = pltpu.roll(t, LANES - HEAD_DIM // 2, 1)
    bwd = pltpu.roll(t, HEAD_DIM // 2, 1)
    return t * cos + fwd * sin_lo + bwd * sin_hi


def _pre_kernel(x_ref, mod_ref, nw_ref, win_ref, qnw_ref, knw_ref, cos_ref, slo_ref, shi_ref,
                g_ref, sgw_ref, sgb_ref, wsp_ref, bsp_ref,
                qlo_ref, qhi_ref, k_ref, vt_ref, ga_ref, s_ref):
    ts = x_ref.shape[1]
    x = x_ref[0]
    shift = mod_ref[0, :, 0:D_MODEL]
    scale = mod_ref[0, :, D_MODEL:2 * D_MODEL]
    xn = x * lax.rsqrt(jnp.mean(x * x, axis=-1, keepdims=True) + EPS) * nw_ref[...]
    hb = (xn * (1.0 + scale) + shift).astype(BF16)

    def proj(idx):
        return jnp.dot(hb, win_ref[:, idx * 512:(idx + 1) * 512], preferred_element_type=F32)

    cos = cos_ref[...]
    slo = slo_ref[...]
    shi = shi_ref[...]
    first_map = lax.broadcasted_iota(jnp.int32, (LANES, ts), 0) < HEAD_DIM

    qn = _chunk_rms(proj(0), g_ref, qnw_ref[...])
    for h in range(N_HEADS):
        qr = _rope_slab(qn[:, h * LANES:(h + 1) * LANES], cos, slo, shi) * (HEAD_DIM ** -0.5 * LOG2E)
        qrt = qr.T
        qlo_ref[0, h] = jnp.where(first_map, qrt, 0.0).astype(BF16)
        qhi_ref[0, h] = jnp.where(first_map, 0.0, qrt).astype(BF16)

    kn = _chunk_rms(proj(1), g_ref, knw_ref[...])
    for h in range(N_HEADS):
        k_ref[0, h] = _rope_slab(kn[:, h * LANES:(h + 1) * LANES], cos, slo, shi).astype(BF16)

    v = proj(2)
    for h in range(N_HEADS):
        vt_ref[0, h, 0:V_DIM, :] = v[:, h * LANES:(h + 1) * LANES].T.astype(BF16)
        vt_ref[0, h, V_DIM:V_ROWS, :] = jnp.ones((V_ROWS - V_DIM, ts), BF16)

    za = proj(3)
    ga_ref[0] = (za * _sigmoid(za)).astype(BF16)

    u = proj(4)
    vg = proj(5)
    zs = proj(6)
    mu = jnp.mean(vg, axis=-1, keepdims=True)
    vc = vg - mu
    var = jnp.mean(vc * vc, axis=-1, keepdims=True)
    vn = (vc * lax.rsqrt(var + EPS) * sgw_ref[...] + sgb_ref[...]).astype(BF16)
    gate = u * (zs * _sigmoid(zs))
    nc = ts // CHUNK
    for g in range(N_SGU_GROUPS):
        gs = slice(g * LANES, (g + 1) * LANES)
        xg = jnp.concatenate([vn[c * CHUNK:(c + 1) * CHUNK, gs] for c in range(nc)], axis=1)
        mg = jnp.dot(wsp_ref[g], xg, preferred_element_type=F32)
        bias = bsp_ref[g]
        for c in range(nc):
            rs = slice(c * CHUNK, (c + 1) * CHUNK)
            s_ref[0, rs, gs] = (gate[rs, gs] * (mg[:, c * LANES:(c + 1) * LANES] + bias)).astype(BF16)


def _pre(x, mod, p, cos, slo, shi):
    b, s, _ = x.shape
    ts = PRE_ROWS
    full = lambda shape: pl.BlockSpec(shape, lambda bi, i: (0,) * len(shape))
    head_out = lambda: pl.BlockSpec((1, N_HEADS, ts, LANES), lambda bi, i: (bi, 0, i, 0))
    return pl.pallas_call(
        _pre_kernel,
        grid=(b, s // ts),
        in_specs=[
            pl.BlockSpec((1, ts, D_MODEL), lambda bi, i: (bi, i, 0)),
            pl.BlockSpec((1, 1, 3 * D_MODEL), lambda bi, i: (bi, 0, 0)),
            full((1, D_MODEL)),
            full((D_MODEL, 7 * 512)),
            full((1, D_ATTN)),
            full((1, D_ATTN)),
            pl.BlockSpec((ts, LANES), lambda bi, i: (i, 0)),
            pl.BlockSpec((ts, LANES), lambda bi, i: (i, 0)),
            pl.BlockSpec((ts, LANES), lambda bi, i: (i, 0)),
            full((MXU_DEPTH, MXU_DEPTH)),
            full((1, D_SGU)),
            full((1, D_SGU)),
            full((N_SGU_GROUPS, CHUNK, CHUNK)),
            full((N_SGU_GROUPS, CHUNK, LANES)),
        ],
        out_specs=[
            pl.BlockSpec((1, N_HEADS, LANES, ts), lambda bi, i: (bi, 0, 0, i)),
            pl.BlockSpec((1, N_HEADS, LANES, ts), lambda bi, i: (bi, 0, 0, i)),
            head_out(),
            pl.BlockSpec((1, N_HEADS, V_ROWS, ts), lambda bi, i: (bi, 0, 0, i)),
            pl.BlockSpec((1, ts, D_ATTN), lambda bi, i: (bi, i, 0)),
            pl.BlockSpec((1, ts, D_SGU), lambda bi, i: (bi, i, 0)),
        ],
        out_shape=[
            jax.ShapeDtypeStruct((b, N_HEADS, LANES, s), BF16),
            jax.ShapeDtypeStruct((b, N_HEADS, LANES, s), BF16),
            jax.ShapeDtypeStruct((b, N_HEADS, s, LANES), BF16),
            jax.ShapeDtypeStruct((b, N_HEADS, V_ROWS, s), BF16),
            jax.ShapeDtypeStruct((b, s, D_ATTN), BF16),
            jax.ShapeDtypeStruct((b, s, D_SGU), BF16),
        ],
        compiler_params=pltpu.CompilerParams(
            dimension_semantics=("arbitrary", "arbitrary"), vmem_limit_bytes=VMEM_LIMIT),
        name="pre",
    )(x, mod, p["norm_w"], p["w_in"], p["q_norm_w"], p["k_norm_w"], cos, slo, shi,
      p["chunk_ones"], p["sgu_norm_w"], p["sgu_norm_b"], p["w_spatial"], p["b_spatial"])


def _attn_kernel(bound_ref, qlo_ref, qhi_ref, k_ref, vt_ref, ga_ref, lqk_ref, subw_ref, o_ref,
                 acc_ref):
    tq = qlo_ref.shape[3]
    s_len = k_ref.shape[2]
    tk = KV_TILE
    n_blocks = s_len // tk
    qdt = jnp.concatenate([qlo_ref[0, 0], qhi_ref[0, 0]], axis=1)

    def scores(blk):
        off = pl.multiple_of(blk * tk, tk)
        kb = k_ref[0, 0, pl.ds(off, tk), :]
        return jnp.dot(kb, qdt, preferred_element_type=F32)

    bound = bound_ref[0]

    def exact_row_max():
        def mbody(j, m):
            return jnp.maximum(m, jnp.max(scores(j), axis=0, keepdims=True))
        return lax.fori_loop(0, n_blocks, mbody, jnp.full((1, 2 * tq), NEG_BIG, F32))

    offset = lax.cond(bound <= SAFE_BOUND, lambda: jnp.full((1, 2 * tq), bound, F32), exact_row_max)

    acc_ref[...] = jnp.zeros(acc_ref.shape, F32)

    def body(it, carry):
        pv = None
        for u in range(KV_UNROLL):
            blk = it * KV_UNROLL + u
            p = jnp.exp2(scores(blk) - offset).astype(BF16)
            vtb = vt_ref[0, 0, :, pl.ds(pl.multiple_of(blk * tk, tk), tk)]
            d = jnp.dot(vtb, p, preferred_element_type=F32)
            pv = d if pv is None else pv + d
        acc_ref[...] += pv
        return carry

    lax.fori_loop(0, n_blocks // KV_UNROLL, body, 0)

    lq = lqk_ref[...]
    lam = (jnp.exp(jnp.sum(lq[0:1] * lq[1:2], axis=1, keepdims=True))
           - jnp.exp(jnp.sum(lq[2:3] * lq[3:4], axis=1, keepdims=True)) + LAMBDA_INIT)
    acc = acc_ref[...]
    on_all = acc[0:V_DIM] * (1.0 / acc[V_DIM:V_DIM + 1])
    o = on_all[:, :tq] - lam * on_all[:, tq:]
    ms = jnp.mean(o * o, axis=0, keepdims=True)
    subw = jnp.concatenate([subw_ref[...]] * (tq // LANES), axis=1)
    on = o * lax.rsqrt(ms + EPS) * subw * (1.0 - LAMBDA_INIT)
    o_ref[0] = (on.T * ga_ref[0].astype(F32)).astype(BF16)


def _score_bound(q_norm_w, k_norm_w):
    bound = (HEAD_DIM ** 0.5 * LOG2E * BOUND_SLACK) * jnp.max(jnp.abs(q_norm_w)) * jnp.max(jnp.abs(k_norm_w))
    return bound.reshape(1).astype(F32)


def _attn(bound, qlo, qhi, k, vt, ga, lqk, subw):
    b, _, s, _ = k.shape
    tq = Q_TILE
    q_spec = lambda: pl.BlockSpec((1, 1, LANES, tq), lambda bi, h, i: (bi, h, 0, i))
    return pl.pallas_call(
        _attn_kernel,
        grid=(b, N_HEADS, s // tq),
        in_specs=[
            pl.BlockSpec(memory_space=pltpu.SMEM),
            q_spec(),
            q_spec(),
            pl.BlockSpec((1, 1, s, LANES), lambda bi, h, i: (bi, h, 0, 0)),
            pl.BlockSpec((1, 1, V_ROWS, s), lambda bi, h, i: (bi, h, 0, 0)),
            pl.BlockSpec((1, tq, V_DIM), lambda bi, h, i: (bi, i, h)),
            pl.BlockSpec((4, HEAD_DIM), lambda bi, h, i: (0, 0)),
            pl.BlockSpec((V_DIM, LANES), lambda bi, h, i: (0, 0)),
        ],
        out_specs=pl.BlockSpec((1, tq, V_DIM), lambda bi, h, i: (bi, i, h)),
        out_shape=jax.ShapeDtypeStruct((b, s, D_ATTN), BF16),
        scratch_shapes=[
            pltpu.VMEM((V_ROWS, 2 * tq), F32),
        ],
        compiler_params=pltpu.CompilerParams(
            dimension_semantics=("arbitrary", "arbitrary", "arbitrary"),
            vmem_limit_bytes=VMEM_LIMIT),
        name="attn",
    )(bound, qlo, qhi, k, vt, ga, lqk, subw)


def _post_kernel(x_ref, a_ref, s_ref, mod_ref, wout_ref, o_ref):
    y = (jnp.dot(a_ref[0], wout_ref[0:D_ATTN, :], preferred_element_type=F32)
         + jnp.dot(s_ref[0], wout_ref[D_ATTN:, :], preferred_element_type=F32))
    gate = mod_ref[0, :, 2 * D_MODEL:3 * D_MODEL]
    o_ref[0] = x_ref[0] + gate * y


def _post(x, a, sg, mod, w_out):
    b, s, _ = x.shape
    ts = POST_ROWS
    return pl.pallas_call(
        _post_kernel,
        grid=(b, s // ts),
        in_specs=[
            pl.BlockSpec((1, ts, D_MODEL), lambda bi, i: (bi, i, 0)),
            pl.BlockSpec((1, ts, D_ATTN), lambda bi, i: (bi, i, 0)),
            pl.BlockSpec((1, ts, D_SGU), lambda bi, i: (bi, i, 0)),
            pl.BlockSpec((1, 1, 3 * D_MODEL), lambda bi, i: (bi, 0, 0)),
            pl.BlockSpec((D_MODEL, D_MODEL), lambda bi, i: (0, 0)),
        ],
        out_specs=pl.BlockSpec((1, ts, D_MODEL), lambda bi, i: (bi, i, 0)),
        out_shape=jax.ShapeDtypeStruct((b, s, D_MODEL), F32),
        compiler_params=pltpu.CompilerParams(
            dimension_semantics=("arbitrary", "arbitrary"), vmem_limit_bytes=VMEM_LIMIT),
        name="post",
    )(x, a, sg, mod, w_out)


def _rope_tables(seq_len):
    half = HEAD_DIM // 2
    reps = LANES // half
    inv_freq = 1.0 / (ROPE_THETA ** (jnp.arange(half, dtype=F32) / half))
    pos = jnp.arange(seq_len, dtype=F32).reshape(seq_len // reps, reps, 1)
    ang = (pos * inv_freq[None, None, :]).reshape(seq_len // reps, LANES)
    cos, sin = lax.optimization_barrier((jnp.cos(ang), jnp.sin(ang)))
    cos = jnp.tile(cos.reshape(seq_len, half), (1, reps))
    sin = jnp.tile(sin.reshape(seq_len, half), (1, reps))
    first_half = (jnp.arange(LANES) % HEAD_DIM) < half
    sin_lo = jnp.where(first_half[None, :], -sin, 0.0)
    sin_hi = jnp.where(first_half[None, :], 0.0, sin)
    return cos, sin_lo, sin_hi


def _layer(x, mod, p, rope):
    qlo, qhi, k, vt, ga, sg = _pre(x, mod, p, *rope)
    a = _attn(p["score_bound"], qlo, qhi, k, vt, ga, p["lambda_qk"], p["subln_w"])
    return _post(x, a, sg, mod, p["w_out"])


def kernel(x_prompt, x_sample, c_prompt, c_sample, norm_w, w_ada, b_ada, w_in, w_out, q_norm_w, k_norm_w, lambda_qk, subln_w, sgu_norm_w, sgu_norm_b, w_spatial, b_spatial):
    depth = norm_w.shape[0]
    assert depth == 1
    nb_p = c_prompt.shape[0]
    nb_s = c_sample.shape[0]
    rows = -(-(nb_p + nb_s) // 8) * 8
    c_all = jnp.concatenate(
        [c_prompt, c_sample, jnp.zeros((rows - nb_p - nb_s, D_MODEL), F32)], axis=0)
    y_prompt, y_sample = x_prompt, x_sample
    for l in range(depth):
        p = {
            "norm_w": norm_w[l].reshape(1, D_MODEL),
            "w_in": w_in[l].astype(BF16),
            "w_out": w_out[l].astype(BF16),
            "q_norm_w": jnp.tile(q_norm_w[l], D_ATTN // HEAD_DIM).reshape(1, D_ATTN),
            "k_norm_w": jnp.tile(k_norm_w[l], D_ATTN // HEAD_DIM).reshape(1, D_ATTN),
            "lambda_qk": lambda_qk[l],
            "score_bound": _score_bound(q_norm_w[l], k_norm_w[l]),
            "subln_w": jnp.broadcast_to(subln_w[l][:, None], (V_DIM, LANES)),
            "sgu_norm_w": sgu_norm_w[l].reshape(1, D_SGU),
            "sgu_norm_b": sgu_norm_b[l].reshape(1, D_SGU),
            "w_spatial": w_spatial[l].astype(BF16),
            "b_spatial": jnp.broadcast_to(b_spatial[l][:, :, None], (N_SGU_GROUPS, CHUNK, LANES)),
            "chunk_ones": jnp.kron(jnp.eye(MXU_DEPTH // HEAD_DIM, dtype=F32),
                                   jnp.ones((HEAD_DIM, HEAD_DIM), F32)).astype(BF16),
        }
        mod = _ada(c_all, w_ada[l], b_ada[l])
        mod_p = mod[0:nb_p].reshape(nb_p, 1, 3 * D_MODEL)
        mod_s = mod[nb_p:nb_p + nb_s].reshape(nb_s, 1, 3 * D_MODEL)
        rope = _rope_tables(max(x_prompt.shape[1], x_sample.shape[1]))
        y_prompt = _layer(y_prompt, mod_p, p, rope)
        y_sample = _layer(y_sample, mod_s, p, rope)
    return (y_prompt, y_sample)
```
